```python
import jax, jax.numpy as jnp
from jax import lax
import numpy as np

D_MODEL = 1024
BATCH = 8
SEQ = 4096
DEPTH = 1

CONV_CH = D_MODEL // 2
CONV_K = 31
CONV_GROUPS = 8
POOL_CH = D_MODEL // 2
POOL_WINDOWS = (2, 4, 8, 16)
N_POOL_GROUPS = len(POOL_WINDOWS)
POOL_GROUP_CH = POOL_CH // N_POOL_GROUPS
POOL_GROUP_OUT = D_MODEL // N_POOL_GROUPS
N_BRANCH = 2
COL_A_VAL = CONV_CH
COL_A_GATE = 2 * CONV_CH
COL_POOL = 2 * CONV_CH + POOL_CH
COL_GATE_A = COL_POOL + D_MODEL
IN_COLS = COL_POOL + N_BRANCH * D_MODEL
D_FF = ((8 * D_MODEL // 3 + 127) // 128) * 128
FFN_K = 3
ALPHA = (2.0 * DEPTH) ** 0.25
BETA = (8.0 * DEPTH) ** -0.25
LN_EPS = 1e-5
N_MOD = 6

kernel_name = "hybrid_conformer_pool_deepnorm_adaln_block"


def layer_norm(x, g=None, b=None):
    xf = x.astype(jnp.float32)
    mu = jnp.mean(xf, axis=-1, keepdims=True)
    var = jnp.mean(jnp.square(xf - mu), axis=-1, keepdims=True)
    y = (xf - mu) * lax.rsqrt(var + LN_EPS)
    if g is not None:
        y = y * g.astype(jnp.float32) + b.astype(jnp.float32)
    return y.astype(x.dtype)


def causal_dwconv(x, w, b):
    k, ch = w.shape
    y = lax.conv_general_dilated(
        x, w[:, None, :].astype(x.dtype), window_strides=(1,),
        padding=[(k - 1, 0)], dimension_numbers=("NWC", "WIO", "NWC"),
        feature_group_count=ch)
    return y + b.astype(x.dtype)


def causal_multiscale_pool(u):
    bsz, s, _ = u.shape
    ug = u.reshape(bsz, s, N_POOL_GROUPS, POOL_GROUP_CH).astype(jnp.float32)
    cs = jnp.cumsum(ug, axis=1)
    pos = jnp.arange(s)
    outs = []
    for gi, w in enumerate(POOL_WINDOWS):
        c_g = cs[:, :, gi]
        lag = jnp.pad(c_g, ((0, 0), (w, 0), (0, 0)))[:, :s]
        cnt = jnp.minimum(pos + 1, w).astype(jnp.float32)[None, :, None]
        outs.append((c_g - lag) / cnt)
    pooled = jnp.stack(outs, axis=2)
    return (pooled - ug).astype(u.dtype)


def setup_inputs(seed: int = 0) -> dict:
    key = jax.random.key(seed)
    ks = jax.random.split(key, 24)
    f32 = jnp.float32
    L = DEPTH

    def nrm(k, shape, scale):
        return jax.random.normal(k, shape, f32) * scale

    def gain(k, shape):
        return 1.0 + 0.05 * jax.random.normal(k, shape, f32)

    return {
        "x": jax.random.normal(ks[0], (BATCH, SEQ, D_MODEL), f32),
        "c": jax.random.normal(ks[1], (BATCH, D_MODEL), f32),
        "w_ada": nrm(ks[2], (L, D_MODEL, N_MOD * D_MODEL), 0.5 * D_MODEL ** -0.5),
        "b_ada": nrm(ks[3], (L, N_MOD * D_MODEL), 0.02),
        "w_in": nrm(ks[4], (L, D_MODEL, IN_COLS), D_MODEL ** -0.5),
        "w_dw_a": nrm(ks[5], (L, CONV_K, CONV_CH), CONV_K ** -0.5),
        "b_dw_a": nrm(ks[6], (L, CONV_CH), 0.02),
        "ln_a_g": gain(ks[7], (L, CONV_CH)),
        "ln_a_b": nrm(ks[8], (L, CONV_CH), 0.02),
        "w_pw_a": nrm(ks[9], (L, CONV_CH, D_MODEL), BETA * CONV_CH ** -0.5),
        "w_pool": nrm(ks[10], (L, N_POOL_GROUPS, POOL_GROUP_CH, POOL_GROUP_OUT), BETA * POOL_GROUP_CH ** -0.5),
        "pool_scale": gain(ks[11], (L, D_MODEL)),
        "w_out": nrm(ks[12], (L, D_MODEL, D_MODEL), BETA * D_MODEL ** -0.5),
        "ln1_g": gain(ks[13], (L, D_MODEL)),
        "ln1_b": nrm(ks[14], (L, D_MODEL), 0.02),
        "w_up": nrm(ks[15], (L, D_MODEL, 2 * D_FF), BETA * D_MODEL ** -0.5),
        "w_dw_f": nrm(ks[16], (L, FFN_K, 2 * D_FF), FFN_K ** -0.5),
        "b_dw_f": nrm(ks[17], (L, 2 * D_FF), 0.02),
        "w_down": nrm(ks[18], (L, D_FF, D_MODEL), BETA * D_FF ** -0.5),
        "ln2_g": gain(ks[19], (L, D_MODEL)),
        "ln2_b": nrm(ks[20], (L, D_MODEL), 0.02),
    }


def reference(x, c, w_ada, b_ada, w_in, w_dw_a, b_dw_a, ln_a_g, ln_a_b, w_pw_a,
              w_pool, pool_scale, w_out, ln1_g, ln1_b, w_up, w_dw_f, b_dw_f,
              w_down, ln2_g, ln2_b):
    bsz, s, _ = x.shape
    for l in range(DEPTH):
        mod = jax.nn.silu(c) @ w_ada[l] + b_ada[l]
        sh1, sc1, g1, sh2, sc2, g2 = jnp.split(mod, N_MOD, axis=-1)

        h = layer_norm(x) * (1.0 + sc1[:, None]) + sh1[:, None]
        proj = h @ w_in[l]
        a_val, a_gate, u_pool, gate_a, gate_b = jnp.split(
            proj, [COL_A_VAL, COL_A_GATE, COL_POOL, COL_GATE_A], axis=-1)

        ya = a_val * jax.nn.sigmoid(a_gate)
        ya = causal_dwconv(ya, w_dw_a[l], b_dw_a[l])
        ya = jax.nn.silu(layer_norm(ya, ln_a_g[l], ln_a_b[l]))
        ya = ya @ w_pw_a[l]

        pooled = causal_multiscale_pool(u_pool)
        yb = jnp.einsum("bsgc,gcd->bsgd", pooled, w_pool[l]).reshape(bsz, s, D_MODEL)
        yb = yb * pool_scale[l]

        merged = jax.nn.sigmoid(gate_a) * ya + jax.nn.sigmoid(gate_b) * yb
        mix = merged @ w_out[l]
        x = layer_norm(ALPHA * x + g1[:, None] * mix, ln1_g[l], ln1_b[l])

        h = layer_norm(x) * (1.0 + sc2[:, None]) + sh2[:, None]
        up = causal_dwconv(h @ w_up[l], w_dw_f[l], b_dw_f[l])
        v, gt = jnp.split(up, 2, axis=-1)
        f = (jax.nn.gelu(gt) * v) @ w_down[l]
        x = layer_norm(ALPHA * x + g2[:, None] * f, ln2_g[l], ln2_b[l])
    return x
```

```python
import functools
import math

import jax
import jax.numpy as jnp
from jax.experimental import pallas as pl
from jax.experimental.pallas import tpu as pltpu

D_MODEL = 1024
DEPTH = 1
CONV_CH = D_MODEL // 2
CONV_K = 31
POOL_CH = D_MODEL // 2
POOL_WINDOWS = (2, 4, 8, 16)
N_POOL_GROUPS = len(POOL_WINDOWS)
POOL_GROUP_CH = POOL_CH // N_POOL_GROUPS
POOL_GROUP_OUT = D_MODEL // N_POOL_GROUPS
COL_A_VAL = CONV_CH
COL_A_GATE = 2 * CONV_CH
COL_POOL = 2 * CONV_CH + POOL_CH
COL_GATE_A = COL_POOL + D_MODEL
IN_COLS = COL_POOL + 2 * D_MODEL
D_FF = ((8 * D_MODEL // 3 + 127) // 128) * 128
FFN_K = 3
ALPHA = (2.0 * DEPTH) ** 0.25
LN_EPS = 1e-5
N_MOD = 6

SUBLANES = 8
LANES = 128
VMEM_LIMIT_BYTES = 56 * 1024 * 1024

SEQ_TILE = 256
CONV_HALO = ((CONV_K - 1 + SUBLANES - 1) // SUBLANES) * SUBLANES
POOL_HALO = ((max(POOL_WINDOWS) - 1 + SUBLANES - 1) // SUBLANES) * SUBLANES
FFN_HALO = SUBLANES
CONV_ROW_CHUNK = 64
FFN_COL_CHUNK = 256

_BF16 = jnp.bfloat16
_F32 = jnp.float32


def _sigmoid(v):
    return 1.0 / (1.0 + jnp.exp(-v))


def _silu(v):
    return v * _sigmoid(v)


def _gelu_tanh(v):
    return 0.5 * v * (1.0 + jnp.tanh(math.sqrt(2.0 / math.pi) * (v + 0.044715 * (v * v * v))))


def _normalize(v):
    mu = jnp.mean(v, axis=-1, keepdims=True)
    d = v - mu
    var = jnp.mean(d * d, axis=-1, keepdims=True)
    return d * jax.lax.rsqrt(var + LN_EPS)


def _dot(a, b):
    return jnp.dot(a, b, preferred_element_type=_F32)


def _adaln_kernel(c_ref, w_ref, b_ref, o_ref):
    c = c_ref[...]
    o_ref[...] = _dot(_silu(c).astype(_BF16), w_ref[...].astype(_BF16)) + b_ref[...]


def _mix_kernel(x_ref, mod_ref, w_in_ref, w_dw_ref, b_dw_ref, lnag_ref, lnab_ref, w_pw_ref,
                w_pool_ref, pscale_ref, w_out_ref, ln1g_ref, ln1b_ref, o_ref,
                ya_buf, u_buf, conv_buf):
    t = SEQ_TILE
    j = pl.program_id(1)

    @pl.when(j == 0)
    def _zero_halos():
        ya_buf[0:CONV_HALO, :] = jnp.zeros((CONV_HALO, CONV_CH), _F32)
        u_buf[0:POOL_HALO, :] = jnp.zeros((POOL_HALO, POOL_CH), _F32)

    x = x_ref[0]
    sh1 = mod_ref[0, 0:1, :]
    sc1 = mod_ref[0, 1:2, :]
    g1 = mod_ref[0, 2:3, :]
    h = (_normalize(x) * (1.0 + sc1) + sh1).astype(_BF16)

    a_val = _dot(h, w_in_ref[:, 0:COL_A_VAL])
    a_gate = _dot(h, w_in_ref[:, COL_A_VAL:COL_A_GATE])
    ya_buf[CONV_HALO:CONV_HALO + t, :] = a_val * _sigmoid(a_gate)
    base = CONV_HALO - (CONV_K - 1)
    for r0 in range(0, t, CONV_ROW_CHUNK):
        for c0 in range(0, CONV_CH, LANES):
            acc = jnp.broadcast_to(b_dw_ref[:, c0:c0 + LANES], (CONV_ROW_CHUNK, LANES))
            for k in range(CONV_K):
                acc = acc + (w_dw_ref[k:k + 1, c0:c0 + LANES]
                             * ya_buf[base + r0 + k:base + r0 + k + CONV_ROW_CHUNK, c0:c0 + LANES])
            conv_buf[r0:r0 + CONV_ROW_CHUNK, c0:c0 + LANES] = acc
    ya_buf[0:CONV_HALO, :] = ya_buf[t:t + CONV_HALO, :]
    ya = _silu(_normalize(conv_buf[...]) * lnag_ref[...] + lnab_ref[...]).astype(_BF16)
    ya = _dot(ya, w_pw_ref[...])

    u_buf[POOL_HALO:POOL_HALO + t, :] = _dot(h, w_in_ref[:, COL_A_GATE:COL_POOL])
    pos = j * t + jax.lax.broadcasted_iota(jnp.int32, (t, 1), 0)
    ybs = []
    for gi, w in enumerate(POOL_WINDOWS):
        c0 = gi * POOL_GROUP_CH
        cur = u_buf[POOL_HALO:POOL_HALO + t, c0:c0 + POOL_GROUP_CH]
        win = cur
        for lag in range(1, w):
            win = win + u_buf[POOL_HALO - lag:POOL_HALO - lag + t, c0:c0 + POOL_GROUP_CH]
        cnt = jnp.minimum(pos + 1, w).astype(_F32)
        pooled = (win / cnt - cur).astype(_BF16)
        ybs.append(_dot(pooled, w_pool_ref[gi]))
    u_buf[0:POOL_HALO, :] = u_buf[t:t + POOL_HALO, :]
    yb = jnp.concatenate(ybs, axis=-1) * pscale_ref[...]

    gate_a = _dot(h, w_in_ref[:, COL_POOL:COL_GATE_A])
    gate_b = _dot(h, w_in_ref[:, COL_GATE_A:IN_COLS])
    merged = (_sigmoid(gate_a) * ya + _sigmoid(gate_b) * yb).astype(_BF16)
    mix = _dot(merged, w_out_ref[...])
    o_ref[0] = _normalize(ALPHA * x + g1 * mix) * ln1g_ref[...] + ln1b_ref[...]


def _ffn_kernel(x_ref, mod_ref, w_up_ref, w_dw_ref, b_dw_ref, w_down_ref, ln2g_ref, ln2b_ref,
                o_ref, up_buf, act_buf):
    t = SEQ_TILE
    j = pl.program_id(1)

    @pl.when(j == 0)
    def _zero_halo():
        up_buf[0:FFN_HALO, :] = jnp.zeros((FFN_HALO, 2 * D_FF), _F32)

    x = x_ref[0]
    sh2 = mod_ref[0, 3:4, :]
    sc2 = mod_ref[0, 4:5, :]
    g2 = mod_ref[0, 5:6, :]
    h = (_normalize(x) * (1.0 + sc2) + sh2).astype(_BF16)

    def conv3(c0):
        cols = slice(c0, c0 + FFN_COL_CHUNK)
        up_buf[FFN_HALO:FFN_HALO + t, cols] = _dot(h, w_up_ref[:, cols])
        acc = jnp.broadcast_to(b_dw_ref[:, cols], (t, FFN_COL_CHUNK))
        for k in range(FFN_K):
            off = FFN_HALO - (FFN_K - 1) + k
            acc = acc + w_dw_ref[k:k + 1, cols] * up_buf[off:off + t, cols]
        up_buf[0:FFN_HALO, cols] = up_buf[t:t + FFN_HALO, cols]
        return acc

    for c0 in range(0, D_FF, FFN_COL_CHUNK):
        v = conv3(c0)
        gt = conv3(D_FF + c0)
        act_buf[:, c0:c0 + FFN_COL_CHUNK] = (_gelu_tanh(gt) * v).astype(_BF16)

    f = _dot(act_buf[...], w_down_ref[...])
    o_ref[0] = _normalize(ALPHA * x + g2 * f) * ln2g_ref[...] + ln2b_ref[...]


def _resident(shape):
    zeros = (0,) * len(shape)
    return pl.BlockSpec(shape, lambda b, j: zeros, pipeline_mode=pl.Buffered(1))


def _row(v):
    return v.reshape(1, -1)


def kernel(x, c, w_ada, b_ada, w_in, w_dw_a, b_dw_a, ln_a_g, ln_a_b, w_pw_a, w_pool, pool_scale,
           w_out, ln1_g, ln1_b, w_up, w_dw_f, b_dw_f, w_down, ln2_g, ln2_b):
    bsz, seq, d = x.shape
    assert d == D_MODEL and seq % SEQ_TILE == 0 and w_ada.shape[0] == DEPTH == 1
    t = SEQ_TILE
    grid = (bsz, seq // t)
    params = pltpu.CompilerParams(dimension_semantics=("arbitrary", "arbitrary"),
                                  vmem_limit_bytes=VMEM_LIMIT_BYTES)
    x_spec = pl.BlockSpec((1, t, d), lambda b, j: (b, j, 0))
    mod_spec = pl.BlockSpec((1, N_MOD, d), lambda b, j: (b, 0, 0))

    mod = pl.pallas_call(
        _adaln_kernel,
        out_shape=jax.ShapeDtypeStruct((bsz, N_MOD * d), _F32),
        grid=(N_MOD,),
        in_specs=[pl.BlockSpec((bsz, d), lambda n: (0, 0)),
                  pl.BlockSpec((d, d), lambda n: (0, n)),
                  pl.BlockSpec((1, d), lambda n: (0, n))],
        out_specs=pl.BlockSpec((bsz, d), lambda n: (0, n)),
        name="adaln_mod",
    )(c, w_ada[0], _row(b_ada[0]))
    mod = mod.reshape(bsz, N_MOD, d)

    x1 = pl.pallas_call(
        _mix_kernel,
        out_shape=jax.ShapeDtypeStruct(x.shape, x.dtype),
        grid=grid,
        in_specs=[x_spec, mod_spec,
                  _resident((d, IN_COLS)), _resident((CONV_K, CONV_CH)), _resident((1, CONV_CH)),
                  _resident((1, CONV_CH)), _resident((1, CONV_CH)), _resident((CONV_CH, d)),
                  _resident((N_POOL_GROUPS, POOL_GROUP_CH, POOL_GROUP_OUT)), _resident((1, d)),
                  _resident((d, d)), _resident((1, d)), _resident((1, d))],
        out_specs=x_spec,
        scratch_shapes=[pltpu.VMEM((CONV_HALO + t, CONV_CH), _F32),
                        pltpu.VMEM((POOL_HALO + t, POOL_CH), _F32),
                        pltpu.VMEM((t, CONV_CH), _F32)],
        compiler_params=params,
        name="token_mix",
    )(x, mod, w_in[0].astype(_BF16), w_dw_a[0], _row(b_dw_a[0]), _row(ln_a_g[0]), _row(ln_a_b[0]),
      w_pw_a[0].astype(_BF16), w_pool[0].astype(_BF16), _row(pool_scale[0]),
      w_out[0].astype(_BF16), _row(ln1_g[0]), _row(ln1_b[0]))

    out = pl.pallas_call(
        _ffn_kernel,
        out_shape=jax.ShapeDtypeStruct(x.shape, x.dtype),
        grid=grid,
        in_specs=[x_spec, mod_spec,
                  _resident((d, 2 * D_FF)), _resident((FFN_K, 2 * D_FF)), _resident((1, 2 * D_FF)),
                  _resident((D_FF, d)), _resident((1, d)), _resident((1, d))],
        out_specs=x_spec,
        scratch_shapes=[pltpu.VMEM((FFN_HALO + t, 2 * D_FF), _F32),
                        pltpu.VMEM((t, D_FF), _BF16)],
        compiler_params=params,
        name="channel_mix",
    )(x1, mod, w_up[0].astype(_BF16), w_dw_f[0], _row(b_dw_f[0]), w_down[0].astype(_BF16),
      _row(ln2_g[0]), _row(ln2_b[0]))
    return out
```

```python
import functools
import math

import jax
import jax.numpy as jnp
from jax.experimental import pallas as pl
from jax.experimental.pallas import tpu as pltpu

D_MODEL = 1024
DEPTH = 1
CONV_CH = D_MODEL // 2
CONV_K = 31
POOL_CH = D_MODEL // 2
POOL_WINDOWS = (2, 4, 8, 16)
N_POOL_GROUPS = len(POOL_WINDOWS)
POOL_GROUP_CH = POOL_CH // N_POOL_GROUPS
POOL_GROUP_OUT = D_MODEL // N_POOL_GROUPS
COL_A_VAL = CONV_CH
COL_A_GATE = 2 * CONV_CH
COL_POOL = 2 * CONV_CH + POOL_CH
COL_GATE_A = COL_POOL + D_MODEL
IN_COLS = COL_POOL + 2 * D_MODEL
D_FF = ((8 * D_MODEL // 3 + 127) // 128) * 128
FFN_K = 3
ALPHA = (2.0 * DEPTH) ** 0.25
LN_EPS = 1e-5
N_MOD = 6

SUBLANES = 8
LANES = 128
MXU_COLS = 256
VMEM_LIMIT_BYTES = 56 * 1024 * 1024

SEQ_TILE = 256
CONV_HALO = ((CONV_K - 1 + SUBLANES - 1) // SUBLANES) * SUBLANES
POOL_HALO = ((max(POOL_WINDOWS) - 1 + SUBLANES - 1) // SUBLANES) * SUBLANES
FFN_HALO = SUBLANES
CONV_ROW_CHUNK = 128
FFN_COL_CHUNK = MXU_COLS
MIX_LAG = 2
FFN_LAG = 1

_BF16 = jnp.bfloat16
_F32 = jnp.float32
_GELU_C = math.sqrt(2.0 / math.pi)

assert POOL_GROUP_CH == LANES and CONV_CH % LANES == 0 and FFN_COL_CHUNK % LANES == 0
assert IN_COLS % MXU_COLS == 0 and SEQ_TILE % CONV_ROW_CHUNK == 0


def _sigmoid(v):
    return 1.0 / (1.0 + jnp.exp(-v))


def _silu(v):
    return v * _sigmoid(v)


def _gelu_tanh(v):
    inner = v * ((v * v) * (_GELU_C * 0.044715) + _GELU_C)
    half = 0.5 * v
    return half * jnp.tanh(inner) + half


def _normalize(v):
    mu = jnp.mean(v, axis=-1, keepdims=True)
    d = v - mu
    var = jnp.mean(d * d, axis=-1, keepdims=True)
    return d * jax.lax.rsqrt(var + LN_EPS)


def _after(value, token):
    u32 = jnp.uint32
    tok = jax.lax.bitcast_convert_type(token, u32)
    zero = jax.lax.shift_right_logical(jax.lax.shift_right_logical(tok, u32(16)), u32(16))
    return jax.lax.bitcast_convert_type(jax.lax.bitcast_convert_type(value, u32) | zero, _F32)


def _dot(a, b):
    return jnp.dot(a, b, preferred_element_type=_F32)


def _adaln_kernel(c_ref, w_ref, b_ref, o_ref):
    c = c_ref[...]
    o_ref[...] = _dot(_silu(c).astype(_BF16), w_ref[...].astype(_BF16)) + b_ref[...]


def _mix_kernel(xa_ref, xc_ref, moda_ref, modc_ref, w_in_ref, w_dw_ref, b_dw_ref, lnag_ref,
                lnab_ref, w_pw_ref, w_pool_ref, pscale_ref, w_out_ref, ln1g_ref, ln1b_ref, o_ref,
                proj_buf, sig_buf, ya_buf, u_buf, conv_buf, merged_buf, *, n_tiles, tiles_per_seq):
    t = SEQ_TILE
    s = pl.program_id(0)
    jb = jnp.clip(s - 1, 0, n_tiles - 1) % tiles_per_seq

    @pl.when(s == 0)
    def _zero_pipeline_state():
        proj_buf[...] = jnp.zeros(proj_buf.shape, _F32)
        merged_buf[...] = jnp.zeros(merged_buf.shape, _BF16)

    @pl.when(jb == 0)
    def _zero_halos():
        ya_buf[:, 0:CONV_HALO, :] = jnp.zeros((CONV_CH // LANES, CONV_HALO, LANES), _F32)
        u_buf[:, 0:POOL_HALO, :] = jnp.zeros((N_POOL_GROUPS, POOL_HALO, LANES), _F32)

    mix = _dot(merged_buf[...], w_out_ref[...])

    glu = proj_buf[:, 0:COL_A_VAL] * _sigmoid(proj_buf[:, COL_A_VAL:COL_A_GATE])
    for ci in range(CONV_CH // LANES):
        ya_buf[ci, CONV_HALO:CONV_HALO + t, :] = glu[:, ci * LANES:(ci + 1) * LANES]
    for gi in range(N_POOL_GROUPS):
        u_buf[gi, POOL_HALO:POOL_HALO + t, :] = proj_buf[:, COL_A_GATE + gi * LANES:
                                                         COL_A_GATE + (gi + 1) * LANES]
    sig_buf[...] = _sigmoid(proj_buf[:, COL_POOL:IN_COLS])

    sh1 = moda_ref[0, 0:1, :]
    sc1 = moda_ref[0, 1:2, :]
    h = (_normalize(xa_ref[0]) * (1.0 + sc1) + sh1).astype(_BF16)

    proj_cols = list(range(0, IN_COLS, MXU_COLS))
    token = [None]

    def last_rows(r):
        return r[t - SUBLANES:t, 0:LANES] + r[t - SUBLANES:t, LANES:2 * LANES]

    def project(n):
        for _ in range(n):
            c0 = proj_cols.pop(0)
            r = _dot(h, w_in_ref[:, c0:c0 + MXU_COLS])
            proj_buf[:, c0:c0 + MXU_COLS] = r
            token[0] = last_rows(r)

    project(2)

    pos = jb * t + jax.lax.broadcasted_iota(jnp.int32, (t, 1), 0)
    for gi, w in enumerate(POOL_WINDOWS):
        cur = u_buf[gi, POOL_HALO:POOL_HALO + t, :]
        win = cur
        for lag in range(1, w):
            win = win + u_buf[gi, POOL_HALO - lag:POOL_HALO - lag + t, :]
        u_buf[gi, 0:POOL_HALO, :] = u_buf[gi, t:t + POOL_HALO, :]
        cnt = jnp.minimum(pos + 1, w).astype(_F32)
        pooled = (win / cnt - cur).astype(_BF16)
        r = _dot(pooled, w_pool_ref[gi])
        cols = slice(gi * POOL_GROUP_OUT, (gi + 1) * POOL_GROUP_OUT)
        gcols = slice(D_MODEL + gi * POOL_GROUP_OUT, D_MODEL + (gi + 1) * POOL_GROUP_OUT)
        sig_buf[:, gcols] = sig_buf[:, gcols] * (r * pscale_ref[:, cols])
        token[0] = token[0] + last_rows(r)

    base = CONV_HALO - (CONV_K - 1)
    n_acc = CONV_ROW_CHUNK // SUBLANES
    tiles_after_chunk = [2, 1, 2, 1, 2, 1, 2, 1]
    for ci in range(CONV_CH // LANES):
        cols = slice(ci * LANES, (ci + 1) * LANES)
        taps = [jnp.broadcast_to(w_dw_ref[k:k + 1, cols], (SUBLANES, LANES)) for k in range(CONV_K)]
        bias = jnp.broadcast_to(b_dw_ref[:, cols], (SUBLANES, LANES))
        for r0 in range(0, t, CONV_ROW_CHUNK):
            acc = [_after(bias, token[0])] * n_acc
            for m in range(CONV_K + SUBLANES * (n_acc - 1)):
                win = ya_buf[ci, base + r0 + m:base + r0 + m + SUBLANES, :]
                for i in range(n_acc):
                    k = m - SUBLANES * i
                    if 0 <= k < CONV_K:
                        acc[i] = acc[i] + taps[k] * win
            conv_buf[r0:r0 + CONV_ROW_CHUNK, cols] = jnp.concatenate(acc, axis=0)
            project(tiles_after_chunk.pop(0))
        ya_buf[ci, 0:CONV_HALO, :] = ya_buf[ci, t:t + CONV_HALO, :]
    assert not proj_cols and not tiles_after_chunk

    g1 = modc_ref[0, 2:3, :]
    out = _normalize(ALPHA * xc_ref[0] + g1 * mix) * ln1g_ref[...] + ln1b_ref[...]
    o_ref[0] = out
    tok_c = out[t - SUBLANES:t, 0:LANES] + token[0]

    lnab = jnp.concatenate([_after(lnab_ref[:, 0:LANES], tok_c[0:1, :]),
                            lnab_ref[:, LANES:CONV_CH]], axis=-1)
    ya = _silu(_normalize(conv_buf[...]) * lnag_ref[...] + lnab).astype(_BF16)
    ya = _dot(ya, w_pw_ref[...])

    merged_buf[...] = (sig_buf[:, 0:D_MODEL] * ya + sig_buf[:, D_MODEL:2 * D_MODEL]).astype(_BF16)


def _ffn_kernel(xa_ref, xb_ref, moda_ref, modb_ref, w_up_ref, w_dw_ref, b_dw_ref, w_down_ref,
                ln2g_ref, ln2b_ref, o_ref, up_buf, act_buf, *, n_tiles, tiles_per_seq):
    t = SEQ_TILE
    s = pl.program_id(0)
    ja = jnp.minimum(s, n_tiles - 1) % tiles_per_seq

    @pl.when(s == 0)
    def _zero_pipeline_state():
        act_buf[...] = jnp.zeros(act_buf.shape, _BF16)

    @pl.when(ja == 0)
    def _zero_halo():
        up_buf[:, 0:FFN_HALO, :] = jnp.zeros((2 * D_FF // LANES, FFN_HALO, LANES), _F32)

    f = _dot(act_buf[...], w_down_ref[...])

    sh2 = moda_ref[0, 3:4, :]
    sc2 = moda_ref[0, 4:5, :]
    h = (_normalize(xa_ref[0]) * (1.0 + sc2) + sh2).astype(_BF16)

    g2 = modb_ref[0, 5:6, :]
    o_ref[0] = _normalize(ALPHA * xb_ref[0] + g2 * f) * ln2g_ref[...] + ln2b_ref[...]

    def conv3(c0):
        up = _dot(h, w_up_ref[:, c0:c0 + FFN_COL_CHUNK])
        outs = []
        for i in range(FFN_COL_CHUNK // LANES):
            cols = slice(c0 + i * LANES, c0 + (i + 1) * LANES)
            slab = c0 // LANES + i
            cur = up[:, i * LANES:(i + 1) * LANES]
            up_buf[slab, FFN_HALO:FFN_HALO + t, :] = cur
            acc = b_dw_ref[:, cols] + w_dw_ref[FFN_K - 1:FFN_K, cols] * cur
            for k in range(FFN_K - 1):
                off = FFN_HALO - (FFN_K - 1) + k
                acc = acc + w_dw_ref[k:k + 1, cols] * up_buf[slab, off:off + t, :]
            up_buf[slab, 0:FFN_HALO, :] = up_buf[slab, t:t + FFN_HALO, :]
            outs.append(acc)
        return jnp.concatenate(outs, axis=-1)

    for c0 in range(0, D_FF, FFN_COL_CHUNK):
        v = conv3(c0)
        gt = conv3(D_FF + c0)
        act_buf[:, c0:c0 + FFN_COL_CHUNK] = (_gelu_tanh(gt) * v).astype(_BF16)


def _resident(shape):
    zeros = (0,) * len(shape)
    return pl.BlockSpec(shape, lambda s: zeros, pipeline_mode=pl.Buffered(1))


def _row(v):
    return v.reshape(1, -1)


def kernel(x, c, w_ada, b_ada, w_in, w_dw_a, b_dw_a, ln_a_g, ln_a_b, w_pw_a, w_pool, pool_scale,
           w_out, ln1_g, ln1_b, w_up, w_dw_f, b_dw_f, w_down, ln2_g, ln2_b):
    bsz, seq, d = x.shape
    assert d == D_MODEL and seq % SEQ_TILE == 0 and w_ada.shape[0] == DEPTH == 1
    t = SEQ_TILE
    tiles_per_seq = seq // t
    n_tiles = bsz * tiles_per_seq
    params = pltpu.CompilerParams(dimension_semantics=("arbitrary",),
                                  vmem_limit_bytes=VMEM_LIMIT_BYTES)

    def x_spec(lag):
        def index(s):
            tile = jnp.clip(s - lag, 0, n_tiles - 1)
            return (tile // tiles_per_seq, tile % tiles_per_seq, 0)
        return pl.BlockSpec((1, t, d), index)

    def mod_spec(lag):
        def index(s):
            return (jnp.clip(s - lag, 0, n_tiles - 1) // tiles_per_seq, 0, 0)
        return pl.BlockSpec((1, N_MOD, d), index)

    mod = pl.pallas_call(
        _adaln_kernel,
        out_shape=jax.ShapeDtypeStruct((bsz, N_MOD * d), _F32),
        grid=(N_MOD,),
        in_specs=[pl.BlockSpec((bsz, d), lambda n: (0, 0)),
                  pl.BlockSpec((d, d), lambda n: (0, n)),
                  pl.BlockSpec((1, d), lambda n: (0, n))],
        out_specs=pl.BlockSpec((bsz, d), lambda n: (0, n)),
        name="adaln_mod",
    )(c, w_ada[0], _row(b_ada[0]))
    mod = mod.reshape(bsz, N_MOD, d)

    x1 = pl.pallas_call(
        functools.partial(_mix_kernel, n_tiles=n_tiles, tiles_per_seq=tiles_per_seq),
        out_shape=jax.ShapeDtypeStruct(x.shape, x.dtype),
        grid=(n_tiles + MIX_LAG,),
        in_specs=[x_spec(0), x_spec(MIX_LAG), mod_spec(0), mod_spec(MIX_LAG),
                  _resident((d, IN_COLS)), _resident((CONV_K, CONV_CH)), _resident((1, CONV_CH)),
                  _resident((1, CONV_CH)), _resident((1, CONV_CH)), _resident((CONV_CH, d)),
                  _resident((N_POOL_GROUPS, POOL_GROUP_CH, POOL_GROUP_OUT)), _resident((1, d)),
                  _resident((d, d)), _resident((1, d)), _resident((1, d))],
        out_specs=x_spec(MIX_LAG),
        scratch_shapes=[pltpu.VMEM((t, IN_COLS), _F32),
                        pltpu.VMEM((t, 2 * d), _F32),
                        pltpu.VMEM((CONV_CH // LANES, CONV_HALO + t, LANES), _F32),
                        pltpu.VMEM((N_POOL_GROUPS, POOL_HALO + t, LANES), _F32),
                        pltpu.VMEM((t, CONV_CH), _F32),
                        pltpu.VMEM((t, d), _BF16)],
        compiler_params=params,
        name="token_mix",
    )(x, x, mod, mod, w_in[0].astype(_BF16), w_dw_a[0], _row(b_dw_a[0]), _row(ln_a_g[0]),
      _row(ln_a_b[0]), w_pw_a[0].astype(_BF16), w_pool[0].astype(_BF16), _row(pool_scale[0]),
      w_out[0].astype(_BF16), _row(ln1_g[0]), _row(ln1_b[0]))

    out = pl.pallas_call(
        functools.partial(_ffn_kernel, n_tiles=n_tiles, tiles_per_seq=tiles_per_seq),
        out_shape=jax.ShapeDtypeStruct(x.shape, x.dtype),
        grid=(n_tiles + FFN_LAG,),
        in_specs=[x_spec(0), x_spec(FFN_LAG), mod_spec(0), mod_spec(FFN_LAG),
                  _resident((d, 2 * D_FF)), _resident((FFN_K, 2 * D_FF)), _resident((1, 2 * D_FF)),
                  _resident((D_FF, d)), _resident((1, d)), _resident((1, d))],
        out_specs=x_spec(FFN_LAG),
        scratch_shapes=[pltpu.VMEM((2 * D_FF // LANES, FFN_HALO + t, LANES), _F32),
                        pltpu.VMEM((t, D_FF), _BF16)],
        compiler_params=params,
        name="channel_mix",
    )(x1, x1, mod, mod, w_up[0].astype(_BF16), w_dw_f[0], _row(b_dw_f[0]),
      w_down[0].astype(_BF16), _row(ln2_g[0]), _row(ln2_b[0]))
    return out
```

```python
import functools
import math

import jax
import jax.numpy as jnp
from jax.experimental import pallas as pl
from jax.experimental.pallas import tpu as pltpu

D_MODEL = 1024
DEPTH = 1
CONV_CH = D_MODEL // 2
CONV_K = 31
POOL_CH = D_MODEL // 2
POOL_WINDOWS = (2, 4, 8, 16)
N_POOL_GROUPS = len(POOL_WINDOWS)
POOL_GROUP_CH = POOL_CH // N_POOL_GROUPS
POOL_GROUP_OUT = D_MODEL // N_POOL_GROUPS
COL_A_VAL = CONV_CH
COL_A_GATE = 2 * CONV_CH
COL_POOL = 2 * CONV_CH + POOL_CH
COL_GATE_A = COL_POOL + D_MODEL
IN_COLS = COL_POOL + 2 * D_MODEL
D_FF = ((8 * D_MODEL // 3 + 127) // 128) * 128
FFN_K = 3
ALPHA = (2.0 * DEPTH) ** 0.25
LN_EPS = 1e-5
N_MOD = 6

SUBLANES = 8
LANES = 128
MXU_COLS = 256
VMEM_LIMIT_BYTES = 56 * 1024 * 1024

SEQ_TILE = 256
CONV_HALO = ((CONV_K - 1 + SUBLANES - 1) // SUBLANES) * SUBLANES
POOL_HALO = ((max(POOL_WINDOWS) - 1 + SUBLANES - 1) // SUBLANES) * SUBLANES
FFN_HALO = SUBLANES
CONV_ROW_CHUNK = 128
FFN_COL_CHUNK = MXU_COLS
MIX_LAG = 2
FFN_LAG = 1

PACK_ROWS = SUBLANES
ROW_CONV_BIAS, ROW_LNA_G, ROW_LNA_B = 0, 1, 2
ROW_POOL_SCALE, ROW_LN1_G, ROW_LN1_B, ROW_LN2_G, ROW_LN2_B = 0, 1, 2, 3, 4
ROW_FFN_BIAS = FFN_K

_BF16 = jnp.bfloat16
_F32 = jnp.float32
_GELU_C = math.sqrt(2.0 / math.pi)

assert POOL_GROUP_CH == LANES and CONV_CH % LANES == 0 and FFN_COL_CHUNK % LANES == 0
assert IN_COLS % MXU_COLS == 0 and SEQ_TILE % CONV_ROW_CHUNK == 0


def _sigmoid(v):
    return 1.0 / (1.0 + jnp.exp(-v))


def _silu(v):
    return v * _sigmoid(v)


def _gelu_tanh(v):
    inner = v * ((v * v) * (_GELU_C * 0.044715) + _GELU_C)
    half = 0.5 * v
    return half * jnp.tanh(inner) + half


def _normalize(v):
    mu = jnp.mean(v, axis=-1, keepdims=True)
    d = v - mu
    var = jnp.mean(d * d, axis=-1, keepdims=True)
    return d * jax.lax.rsqrt(var + LN_EPS)


def _after(value, token):
    u32 = jnp.uint32
    tok = jax.lax.bitcast_convert_type(token, u32)
    zero = jax.lax.shift_right_logical(jax.lax.shift_right_logical(tok, u32(16)), u32(16))
    return jax.lax.bitcast_convert_type(jax.lax.bitcast_convert_type(value, u32) | zero, _F32)


def _dot(a, b):
    return jnp.dot(a, b, preferred_element_type=_F32)


def _param_row(ref, row, cols=slice(None)):
    return ref[row:row + 1, cols]


def _adaln_kernel(c_ref, w_ref, b_ref, o_ref):
    c = c_ref[...]
    o_ref[...] = _dot(_silu(c).astype(_BF16), w_ref[...].astype(_BF16)) + b_ref[...]


def _mix_kernel(xa_ref, xc_ref, moda_ref, modc_ref, w_in_ref, w_dw_ref, rows_a_ref, w_pw_ref,
                w_pool_ref, rows_d_ref, w_out_ref, o_ref,
                proj_buf, sig_buf, ya_buf, u_buf, conv_buf, merged_buf, *, n_tiles, tiles_per_seq):
    t = SEQ_TILE
    s = pl.program_id(0)
    jb = jnp.clip(s - 1, 0, n_tiles - 1) % tiles_per_seq

    @pl.when(s == 0)
    def _zero_pipeline_state():
        proj_buf[...] = jnp.zeros(proj_buf.shape, _F32)
        merged_buf[...] = jnp.zeros(merged_buf.shape, _BF16)

    @pl.when(jb == 0)
    def _zero_halos():
        ya_buf[:, 0:CONV_HALO, :] = jnp.zeros((CONV_CH // LANES, CONV_HALO, LANES), _F32)
        u_buf[:, 0:POOL_HALO, :] = jnp.zeros((N_POOL_GROUPS, POOL_HALO, LANES), _F32)

    mix = _dot(merged_buf[...], w_out_ref[...])

    glu = proj_buf[:, 0:COL_A_VAL] * _sigmoid(proj_buf[:, COL_A_VAL:COL_A_GATE])
    for ci in range(CONV_CH // LANES):
        ya_buf[ci, CONV_HALO:CONV_HALO + t, :] = glu[:, ci * LANES:(ci + 1) * LANES]
    for gi in range(N_POOL_GROUPS):
        u_buf[gi, POOL_HALO:POOL_HALO + t, :] = proj_buf[:, COL_A_GATE + gi * LANES:
                                                         COL_A_GATE + (gi + 1) * LANES]
    sig_buf[...] = _sigmoid(proj_buf[:, COL_POOL:IN_COLS])

    sh1 = moda_ref[0, 0:1, :]
    sc1 = moda_ref[0, 1:2, :]
    h = (_normalize(xa_ref[0]) * (1.0 + sc1) + sh1).astype(_BF16)

    proj_cols = list(range(0, IN_COLS, MXU_COLS))
    token = [None]

    def last_rows(r):
        return r[t - SUBLANES:t, 0:LANES] + r[t - SUBLANES:t, LANES:2 * LANES]

    def project(n):
        for _ in range(n):
            c0 = proj_cols.pop(0)
            r = _dot(h, w_in_ref[:, c0:c0 + MXU_COLS])
            proj_buf[:, c0:c0 + MXU_COLS] = r
            token[0] = last_rows(r)

    project(2)

    pos = jb * t + jax.lax.broadcasted_iota(jnp.int32, (t, 1), 0)
    for gi, w in enumerate(POOL_WINDOWS):
        cur = u_buf[gi, POOL_HALO:POOL_HALO + t, :]
        win = cur
        for lag in range(1, w):
            win = win + u_buf[gi, POOL_HALO - lag:POOL_HALO - lag + t, :]
        u_buf[gi, 0:POOL_HALO, :] = u_buf[gi, t:t + POOL_HALO, :]
        cnt = jnp.minimum(pos + 1, w).astype(_F32)
        pooled = (win / cnt - cur).astype(_BF16)
        r = _dot(pooled, w_pool_ref[gi])
        cols = slice(gi * POOL_GROUP_OUT, (gi + 1) * POOL_GROUP_OUT)
        gcols = slice(D_MODEL + gi * POOL_GROUP_OUT, D_MODEL + (gi + 1) * POOL_GROUP_OUT)
        sig_buf[:, gcols] = sig_buf[:, gcols] * (r * _param_row(rows_d_ref, ROW_POOL_SCALE, cols))
        token[0] = token[0] + last_rows(r)

    base = CONV_HALO - (CONV_K - 1)
    n_acc = CONV_ROW_CHUNK // SUBLANES
    tiles_after_chunk = [2, 1, 2, 1, 2, 1, 2, 1]
    for ci in range(CONV_CH // LANES):
        cols = slice(ci * LANES, (ci + 1) * LANES)
        taps = [jnp.broadcast_to(w_dw_ref[k:k + 1, cols], (SUBLANES, LANES)) for k in range(CONV_K)]
        bias = jnp.broadcast_to(_param_row(rows_a_ref, ROW_CONV_BIAS, cols), (SUBLANES, LANES))
        for r0 in range(0, t, CONV_ROW_CHUNK):
            acc = [_after(bias, token[0])] * n_acc
            for m in range(CONV_K + SUBLANES * (n_acc - 1)):
                win = ya_buf[ci, base + r0 + m:base + r0 + m + SUBLANES, :]
                for i in range(n_acc):
                    k = m - SUBLANES * i
                    if 0 <= k < CONV_K:
                        acc[i] = acc[i] + taps[k] * win
            conv_buf[r0:r0 + CONV_ROW_CHUNK, cols] = jnp.concatenate(acc, axis=0)
            project(tiles_after_chunk.pop(0))
        ya_buf[ci, 0:CONV_HALO, :] = ya_buf[ci, t:t + CONV_HALO, :]
    assert not proj_cols and not tiles_after_chunk

    g1 = modc_ref[0, 2:3, :]
    out = (_normalize(ALPHA * xc_ref[0] + g1 * mix) * _param_row(rows_d_ref, ROW_LN1_G)
           + _param_row(rows_d_ref, ROW_LN1_B))
    o_ref[0] = out
    tok_c = out[t - SUBLANES:t, 0:LANES] + token[0]

    lnab = jnp.concatenate(
        [_after(_param_row(rows_a_ref, ROW_LNA_B, slice(0, LANES)), tok_c[0:1, :]),
         _param_row(rows_a_ref, ROW_LNA_B, slice(LANES, CONV_CH))], axis=-1)
    ya = _silu(_normalize(conv_buf[...]) * _param_row(rows_a_ref, ROW_LNA_G) + lnab).astype(_BF16)
    ya = _dot(ya, w_pw_ref[...])

    merged_buf[...] = (sig_buf[:, 0:D_MODEL] * ya + sig_buf[:, D_MODEL:2 * D_MODEL]).astype(_BF16)


def _ffn_kernel(xa_ref, xb_ref, moda_ref, modb_ref, w_up_ref, rows_f_ref, w_down_ref, rows_d_ref,
                o_ref, up_buf, act_buf, *, n_tiles, tiles_per_seq):
    t = SEQ_TILE
    s = pl.program_id(0)
    ja = jnp.minimum(s, n_tiles - 1) % tiles_per_seq

    @pl.when(s == 0)
    def _zero_pipeline_state():
        act_buf[...] = jnp.zeros(act_buf.shape, _BF16)

    @pl.when(ja == 0)
    def _zero_halo():
        up_buf[:, 0:FFN_HALO, :] = jnp.zeros((2 * D_FF // LANES, FFN_HALO, LANES), _F32)

    f = _dot(act_buf[...], w_down_ref[...])

    sh2 = moda_ref[0, 3:4, :]
    sc2 = moda_ref[0, 4:5, :]
    h = (_normalize(xa_ref[0]) * (1.0 + sc2) + sh2).astype(_BF16)

    g2 = modb_ref[0, 5:6, :]
    o_ref[0] = (_normalize(ALPHA * xb_ref[0] + g2 * f) * _param_row(rows_d_ref, ROW_LN2_G)
                + _param_row(rows_d_ref, ROW_LN2_B))

    def conv3(c0):
        up = _dot(h, w_up_ref[:, c0:c0 + FFN_COL_CHUNK])
        outs = []
        for i in range(FFN_COL_CHUNK // LANES):
            cols = slice(c0 + i * LANES, c0 + (i + 1) * LANES)
            slab = c0 // LANES + i
            cur = up[:, i * LANES:(i + 1) * LANES]
            up_buf[slab, FFN_HALO:FFN_HALO + t, :] = cur
            acc = (_param_row(rows_f_ref, ROW_FFN_BIAS, cols)
                   + _param_row(rows_f_ref, FFN_K - 1, cols) * cur)
            for k in range(FFN_K - 1):
                off = FFN_HALO - (FFN_K - 1) + k
                acc = acc + _param_row(rows_f_ref, k, cols) * up_buf[slab, off:off + t, :]
            up_buf[slab, 0:FFN_HALO, :] = up_buf[slab, t:t + FFN_HALO, :]
            outs.append(acc)
        return jnp.concatenate(outs, axis=-1)

    for c0 in range(0, D_FF, FFN_COL_CHUNK):
        v = conv3(c0)
        gt = conv3(D_FF + c0)
        act_buf[:, c0:c0 + FFN_COL_CHUNK] = (_gelu_tanh(gt) * v).astype(_BF16)


def _resident(shape):
    zeros = (0,) * len(shape)
    return pl.BlockSpec(shape, lambda s: zeros, pipeline_mode=pl.Buffered(1))


def _row(v):
    return v.reshape(1, -1)


def _pack_rows(rows):
    rows = [_row(r) for r in rows]
    pad = jnp.zeros((PACK_ROWS - len(rows), rows[0].shape[1]), _F32)
    return jnp.concatenate(rows + [pad], axis=0)


def kernel(x, c, w_ada, b_ada, w_in, w_dw_a, b_dw_a, ln_a_g, ln_a_b, w_pw_a, w_pool, pool_scale,
           w_out, ln1_g, ln1_b, w_up, w_dw_f, b_dw_f, w_down, ln2_g, ln2_b):
    bsz, seq, d = x.shape
    assert d == D_MODEL and seq % SEQ_TILE == 0 and w_ada.shape[0] == DEPTH == 1
    t = SEQ_TILE
    tiles_per_seq = seq // t
    n_tiles = bsz * tiles_per_seq
    params = pltpu.CompilerParams(dimension_semantics=("arbitrary",),
                                  vmem_limit_bytes=VMEM_LIMIT_BYTES)

    def x_spec(lag):
        def index(s):
            tile = jnp.clip(s - lag, 0, n_tiles - 1)
            return (tile // tiles_per_seq, tile % tiles_per_seq, 0)
        return pl.BlockSpec((1, t, d), index)

    def mod_spec(lag):
        def index(s):
            return (jnp.clip(s - lag, 0, n_tiles - 1) // tiles_per_seq, 0, 0)
        return pl.BlockSpec((1, N_MOD, d), index)

    mod = pl.pallas_call(
        _adaln_kernel,
        out_shape=jax.ShapeDtypeStruct((bsz, N_MOD * d), _F32),
        grid=(N_MOD,),
        in_specs=[pl.BlockSpec((bsz, d), lambda n: (0, 0)),
                  pl.BlockSpec((d, d), lambda n: (0, n)),
                  pl.BlockSpec((1, d), lambda n: (0, n))],
        out_specs=pl.BlockSpec((bsz, d), lambda n: (0, n)),
        name="adaln_mod",
    )(c, w_ada[0], _row(b_ada[0]))
    mod = mod.reshape(bsz, N_MOD, d)

    rows_a = _pack_rows([b_dw_a[0], ln_a_g[0], ln_a_b[0]])
    rows_d = _pack_rows([pool_scale[0], ln1_g[0], ln1_b[0], ln2_g[0], ln2_b[0]])
    rows_f = _pack_rows([w_dw_f[0, k] for k in range(FFN_K)] + [b_dw_f[0]])

    x1 = pl.pallas_call(
        functools.partial(_mix_kernel, n_tiles=n_tiles, tiles_per_seq=tiles_per_seq),
        out_shape=jax.ShapeDtypeStruct(x.shape, x.dtype),
        grid=(n_tiles + MIX_LAG,),
        in_specs=[x_spec(0), x_spec(MIX_LAG), mod_spec(0), mod_spec(MIX_LAG),
                  _resident((d, IN_COLS)), _resident((CONV_K, CONV_CH)),
                  _resident((PACK_ROWS, CONV_CH)), _resident((CONV_CH, d)),
                  _resident((N_POOL_GROUPS, POOL_GROUP_CH, POOL_GROUP_OUT)),
                  _resident((PACK_ROWS, d)), _resident((d, d))],
        out_specs=x_spec(MIX_LAG),
        scratch_shapes=[pltpu.VMEM((t, IN_COLS), _F32),
                        pltpu.VMEM((t, 2 * d), _F32),
                        pltpu.VMEM((CONV_CH // LANES, CONV_HALO + t, LANES), _F32),
                        pltpu.VMEM((N_POOL_GROUPS, POOL_HALO + t, LANES), _F32),
                        pltpu.VMEM((t, CONV_CH), _F32),
                        pltpu.VMEM((t, d), _BF16)],
        compiler_params=params,
        name="token_mix",
    )(x, x, mod, mod, w_in[0].astype(_BF16), w_dw_a[0], rows_a, w_pw_a[0].astype(_BF16),
      w_pool[0].astype(_BF16), rows_d, w_out[0].astype(_BF16))

    out = pl.pallas_call(
        functools.partial(_ffn_kernel, n_tiles=n_tiles, tiles_per_seq=tiles_per_seq),
        out_shape=jax.ShapeDtypeStruct(x.shape, x.dtype),
        grid=(n_tiles + FFN_LAG,),
        in_specs=[x_spec(0), x_spec(FFN_LAG), mod_spec(0), mod_spec(FFN_LAG),
                  _resident((d, 2 * D_FF)), _resident((PACK_ROWS, 2 * D_FF)),
                  _resident((D_FF, d)), _resident((PACK_ROWS, d))],
        out_specs=x_spec(FFN_LAG),
        scratch_shapes=[pltpu.VMEM((2 * D_FF // LANES, FFN_HALO + t, LANES), _F32),
                        pltpu.VMEM((t, D_FF), _BF16)],
        compiler_params=params,
        name="channel_mix",
    )(x1, x1, mod, mod, w_up[0].astype(_BF16), rows_f, w_down[0].astype(_BF16), rows_d)
    return out
```

```python
import functools
import math

import jax
import jax.numpy as jnp
from jax.experimental import pallas as pl
from jax.experimental.pallas import tpu as pltpu

D_MODEL = 1024
DEPTH = 1
CONV_CH = D_MODEL // 2
CONV_K = 31
POOL_CH = D_MODEL // 2
POOL_WINDOWS = (2, 4, 8, 16)
N_POOL_GROUPS = len(POOL_WINDOWS)
POOL_GROUP_CH = POOL_CH // N_POOL_GROUPS
POOL_GROUP_OUT = D_MODEL // N_POOL_GROUPS
COL_A_VAL = CONV_CH
COL_A_GATE = 2 * CONV_CH
COL_POOL = 2 * CONV_CH + POOL_CH
COL_GATE_A = COL_POOL + D_MODEL
IN_COLS = COL_POOL + 2 * D_MODEL
D_FF = ((8 * D_MODEL // 3 + 127) // 128) * 128
FFN_K = 3
ALPHA = (2.0 * DEPTH) ** 0.25
LN_EPS = 1e-5
N_MOD = 6

SUBLANES = 8
LANES = 128
MXU_COLS = 256
VMEM_LIMIT_BYTES = 56 * 1024 * 1024

MIX_TILE = 512
FFN_TILE = 256
CONV_HALO = ((CONV_K - 1 + SUBLANES - 1) // SUBLANES) * SUBLANES
POOL_HALO = ((max(POOL_WINDOWS) - 1 + SUBLANES - 1) // SUBLANES) * SUBLANES
FFN_HALO = SUBLANES
CONV_ROW_CHUNK = 128
FFN_COL_CHUNK = MXU_COLS
MIX_LAG = 2
FFN_LAG = 1

PACK_ROWS = SUBLANES
ROW_CONV_BIAS, ROW_LNA_G, ROW_LNA_B = 0, 1, 2
ROW_POOL_SCALE, ROW_LN1_G, ROW_LN1_B, ROW_LN2_G, ROW_LN2_B = 0, 1, 2, 3, 4
ROW_FFN_BIAS = FFN_K

_BF16 = jnp.bfloat16
_F32 = jnp.float32
_GELU_C = math.sqrt(2.0 / math.pi)

assert POOL_GROUP_CH == LANES and CONV_CH % LANES == 0 and FFN_COL_CHUNK % LANES == 0
assert IN_COLS % MXU_COLS == 0 and MIX_TILE % CONV_ROW_CHUNK == 0


def _sigmoid(v):
    return 1.0 / (1.0 + jnp.exp(-v))


def _silu(v):
    return v * _sigmoid(v)


def _gelu_tanh(v):
    inner = v * ((v * v) * (_GELU_C * 0.044715) + _GELU_C)
    half = 0.5 * v
    return half * jnp.tanh(inner) + half


def _normalize(v):
    mu = jnp.mean(v, axis=-1, keepdims=True)
    d = v - mu
    var = jnp.mean(d * d, axis=-1, keepdims=True)
    return d * jax.lax.rsqrt(var + LN_EPS)


def _after(value, token):
    u32 = jnp.uint32
    tok = jax.lax.bitcast_convert_type(token, u32)
    zero = jax.lax.shift_right_logical(jax.lax.shift_right_logical(tok, u32(16)), u32(16))
    return jax.lax.bitcast_convert_type(jax.lax.bitcast_convert_type(value, u32) | zero, _F32)


def _dot(a, b):
    return jnp.dot(a, b, preferred_element_type=_F32)


def _param_row(ref, row, cols=slice(None)):
    return ref[row:row + 1, cols]


def _adaln_kernel(c_ref, w_ref, b_ref, o_ref):
    c = c_ref[...]
    o_ref[...] = _dot(_silu(c).astype(_BF16), w_ref[...].astype(_BF16)) + b_ref[...]


def _mix_kernel(xa_ref, xc_ref, moda_ref, modc_ref, w_in_ref, w_dw_ref, rows_a_ref, w_pw_ref,
                w_pool_ref, rows_d_ref, w_out_ref, o_ref,
                proj_buf, sig_buf, ya_buf, u_buf, conv_buf, merged_buf, *, n_tiles, tiles_per_seq):
    t = MIX_TILE
    s = pl.program_id(0)
    jb = jnp.clip(s - 1, 0, n_tiles - 1) % tiles_per_seq

    @pl.when(s == 0)
    def _zero_pipeline_state():
        proj_buf[...] = jnp.zeros(proj_buf.shape, _F32)
        merged_buf[...] = jnp.zeros(merged_buf.shape, _BF16)

    @pl.when(jb == 0)
    def _zero_halos():
        ya_buf[:, 0:CONV_HALO, :] = jnp.zeros((CONV_CH // LANES, CONV_HALO, LANES), _F32)
        u_buf[:, 0:POOL_HALO, :] = jnp.zeros((N_POOL_GROUPS, POOL_HALO, LANES), _F32)

    mix = _dot(merged_buf[...], w_out_ref[...])

    glu = proj_buf[:, 0:COL_A_VAL] * _sigmoid(proj_buf[:, COL_A_VAL:COL_A_GATE])
    for ci in range(CONV_CH // LANES):
        ya_buf[ci, CONV_HALO:CONV_HALO + t, :] = glu[:, ci * LANES:(ci + 1) * LANES]
    for gi in range(N_POOL_GROUPS):
        u_buf[gi, POOL_HALO:POOL_HALO + t, :] = proj_buf[:, COL_A_GATE + gi * LANES:
                                                         COL_A_GATE + (gi + 1) * LANES]
    sig_buf[...] = _sigmoid(proj_buf[:, COL_POOL:IN_COLS])

    sh1 = moda_ref[0, 0:1, :]
    sc1 = moda_ref[0, 1:2, :]
    h = (_normalize(xa_ref[0]) * (1.0 + sc1) + sh1).astype(_BF16)

    proj_cols = list(range(0, IN_COLS, MXU_COLS))
    token = [None]

    def last_rows(r):
        return r[t - SUBLANES:t, 0:LANES] + r[t - SUBLANES:t, LANES:2 * LANES]

    def project(n):
        for _ in range(n):
            c0 = proj_cols.pop(0)
            r = _dot(h, w_in_ref[:, c0:c0 + MXU_COLS])
            proj_buf[:, c0:c0 + MXU_COLS] = r
            token[0] = last_rows(r)

    project(2)

    pos = jb * t + jax.lax.broadcasted_iota(jnp.int32, (t, 1), 0)
    for gi, w in enumerate(POOL_WINDOWS):
        cur = u_buf[gi, POOL_HALO:POOL_HALO + t, :]
        win = cur
        for lag in range(1, w):
            win = win + u_buf[gi, POOL_HALO - lag:POOL_HALO - lag + t, :]
        u_buf[gi, 0:POOL_HALO, :] = u_buf[gi, t:t + POOL_HALO, :]
        cnt = jnp.minimum(pos + 1, w).astype(_F32)
        pooled = (win / cnt - cur).astype(_BF16)
        r = _dot(pooled, w_pool_ref[gi])
        cols = slice(gi * POOL_GROUP_OUT, (gi + 1) * POOL_GROUP_OUT)
        gcols = slice(D_MODEL + gi * POOL_GROUP_OUT, D_MODEL + (gi + 1) * POOL_GROUP_OUT)
        sig_buf[:, gcols] = sig_buf[:, gcols] * (r * _param_row(rows_d_ref, ROW_POOL_SCALE, cols))
        token[0] = token[0] + last_rows(r)

    base = CONV_HALO - (CONV_K - 1)
    n_acc = CONV_ROW_CHUNK // SUBLANES
    n_chunks = (CONV_CH // LANES) * (t // CONV_ROW_CHUNK)
    n_rest = IN_COLS // MXU_COLS - 2
    tiles_after_chunk = [(k + 1) * n_rest // n_chunks - k * n_rest // n_chunks
                         for k in range(n_chunks)]
    for ci in range(CONV_CH // LANES):
        cols = slice(ci * LANES, (ci + 1) * LANES)
        taps = [jnp.broadcast_to(w_dw_ref[k:k + 1, cols], (SUBLANES, LANES)) for k in range(CONV_K)]
        bias = jnp.broadcast_to(_param_row(rows_a_ref, ROW_CONV_BIAS, cols), (SUBLANES, LANES))
        for r0 in range(0, t, CONV_ROW_CHUNK):
            acc = [_after(bias, token[0])] * n_acc
            for m in range(CONV_K + SUBLANES * (n_acc - 1)):
                win = ya_buf[ci, base + r0 + m:base + r0 + m + SUBLANES, :]
                for i in range(n_acc):
                    k = m - SUBLANES * i
                    if 0 <= k < CONV_K:
                        acc[i] = acc[i] + taps[k] * win
            conv_buf[r0:r0 + CONV_ROW_CHUNK, cols] = jnp.concatenate(acc, axis=0)
            project(tiles_after_chunk.pop(0))
        ya_buf[ci, 0:CONV_HALO, :] = ya_buf[ci, t:t + CONV_HALO, :]
    assert not proj_cols and not tiles_after_chunk

    g1 = modc_ref[0, 2:3, :]
    out = (_normalize(ALPHA * xc_ref[0] + g1 * mix) * _param_row(rows_d_ref, ROW_LN1_G)
           + _param_row(rows_d_ref, ROW_LN1_B))
    o_ref[0] = out
    tok_c = out[t - SUBLANES:t, 0:LANES] + token[0]

    lnab = jnp.concatenate(
        [_after(_param_row(rows_a_ref, ROW_LNA_B, slice(0, LANES)), tok_c[0:1, :]),
         _param_row(rows_a_ref, ROW_LNA_B, slice(LANES, CONV_CH))], axis=-1)
    ya = _silu(_normalize(conv_buf[...]) * _param_row(rows_a_ref, ROW_LNA_G) + lnab).astype(_BF16)
    ya = _dot(ya, w_pw_ref[...])

    merged_buf[...] = (sig_buf[:, 0:D_MODEL] * ya + sig_buf[:, D_MODEL:2 * D_MODEL]).astype(_BF16)


def _ffn_kernel(xa_ref, xb_ref, moda_ref, modb_ref, w_up_ref, rows_f_ref, w_down_ref, rows_d_ref,
                o_ref, up_buf, act_buf, *, n_tiles, tiles_per_seq):
    t = FFN_TILE
    s = pl.program_id(0)
    ja = jnp.minimum(s, n_tiles - 1) % tiles_per_seq

    @pl.when(s == 0)
    def _zero_pipeline_state():
        act_buf[...] = jnp.zeros(act_buf.shape, _BF16)

    @pl.when(ja == 0)
    def _zero_halo():
        up_buf[:, 0:FFN_HALO, :] = jnp.zeros((2 * D_FF // LANES, FFN_HALO, LANES), _F32)

    f = _dot(act_buf[...], w_down_ref[...])

    sh2 = moda_ref[0, 3:4, :]
    sc2 = moda_ref[0, 4:5, :]
    h = (_normalize(xa_ref[0]) * (1.0 + sc2) + sh2).astype(_BF16)

    g2 = modb_ref[0, 5:6, :]
    o_ref[0] = (_normalize(ALPHA * xb_ref[0] + g2 * f) * _param_row(rows_d_ref, ROW_LN2_G)
                + _param_row(rows_d_ref, ROW_LN2_B))

    def conv3(c0):
        up = _dot(h, w_up_ref[:, c0:c0 + FFN_COL_CHUNK])
        outs = []
        for i in range(FFN_COL_CHUNK // LANES):
            cols = slice(c0 + i * LANES, c0 + (i + 1) * LANES)
            slab = c0 // LANES + i
            cur = up[:, i * LANES:(i + 1) * LANES]
            up_buf[slab, FFN_HALO:FFN_HALO + t, :] = cur
            acc = (_param_row(rows_f_ref, ROW_FFN_BIAS, cols)
                   + _param_row(rows_f_ref, FFN_K - 1, cols) * cur)
            for k in range(FFN_K - 1):
                off = FFN_HALO - (FFN_K - 1) + k
                acc = acc + _param_row(rows_f_ref, k, cols) * up_buf[slab, off:off + t, :]
            up_buf[slab, 0:FFN_HALO, :] = up_buf[slab, t:t + FFN_HALO, :]
            outs.append(acc)
        return jnp.concatenate(outs, axis=-1)

    for c0 in range(0, D_FF, FFN_COL_CHUNK):
        v = conv3(c0)
        gt = conv3(D_FF + c0)
        act_buf[:, c0:c0 + FFN_COL_CHUNK] = (_gelu_tanh(gt) * v).astype(_BF16)


def _resident(shape):
    zeros = (0,) * len(shape)
    return pl.BlockSpec(shape, lambda s: zeros, pipeline_mode=pl.Buffered(1))


def _row(v):
    return v.reshape(1, -1)


def _pack_rows(rows):
    rows = [_row(r) for r in rows]
    pad = jnp.zeros((PACK_ROWS - len(rows), rows[0].shape[1]), _F32)
    return jnp.concatenate(rows + [pad], axis=0)


def kernel(x, c, w_ada, b_ada, w_in, w_dw_a, b_dw_a, ln_a_g, ln_a_b, w_pw_a, w_pool, pool_scale,
           w_out, ln1_g, ln1_b, w_up, w_dw_f, b_dw_f, w_down, ln2_g, ln2_b):
    bsz, seq, d = x.shape
    assert d == D_MODEL and seq % MIX_TILE == 0 and seq % FFN_TILE == 0
    assert w_ada.shape[0] == DEPTH == 1
    params = pltpu.CompilerParams(dimension_semantics=("arbitrary",),
                                  vmem_limit_bytes=VMEM_LIMIT_BYTES)

    def tiling(t):
        tiles_per_seq = seq // t
        n_tiles = bsz * tiles_per_seq

        def x_spec(lag):
            def index(s):
                tile = jnp.clip(s - lag, 0, n_tiles - 1)
                return (tile // tiles_per_seq, tile % tiles_per_seq, 0)
            return pl.BlockSpec((1, t, d), index)

        def mod_spec(lag):
            def index(s):
                return (jnp.clip(s - lag, 0, n_tiles - 1) // tiles_per_seq, 0, 0)
            return pl.BlockSpec((1, N_MOD, d), index)

        return tiles_per_seq, n_tiles, x_spec, mod_spec

    mod = pl.pallas_call(
        _adaln_kernel,
        out_shape=jax.ShapeDtypeStruct((bsz, N_MOD * d), _F32),
        grid=(N_MOD,),
        in_specs=[pl.BlockSpec((bsz, d), lambda n: (0, 0)),
                  pl.BlockSpec((d, d), lambda n: (0, n)),
                  pl.BlockSpec((1, d), lambda n: (0, n))],
        out_specs=pl.BlockSpec((bsz, d), lambda n: (0, n)),
        name="adaln_mod",
    )(c, w_ada[0], _row(b_ada[0]))
    mod = mod.reshape(bsz, N_MOD, d)

    rows_a = _pack_rows([b_dw_a[0], ln_a_g[0], ln_a_b[0]])
    rows_d = _pack_rows([pool_scale[0], ln1_g[0], ln1_b[0], ln2_g[0], ln2_b[0]])
    rows_f = _pack_rows([w_dw_f[0, k] for k in range(FFN_K)] + [b_dw_f[0]])

    t = MIX_TILE
    tiles_per_seq, n_tiles, x_spec, mod_spec = tiling(t)
    x1 = pl.pallas_call(
        functools.partial(_mix_kernel, n_tiles=n_tiles, tiles_per_seq=tiles_per_seq),
        out_shape=jax.ShapeDtypeStruct(x.shape, x.dtype),
        grid=(n_tiles + MIX_LAG,),
        in_specs=[x_spec(0), x_spec(MIX_LAG), mod_spec(0), mod_spec(MIX_LAG),
                  _resident((d, IN_COLS)), _resident((CONV_K, CONV_CH)),
                  _resident((PACK_ROWS, CONV_CH)), _resident((CONV_CH, d)),
                  _resident((N_POOL_GROUPS, POOL_GROUP_CH, POOL_GROUP_OUT)),
                  _resident((PACK_ROWS, d)), _resident((d, d))],
        out_specs=x_spec(MIX_LAG),
        scratch_shapes=[pltpu.VMEM((t, IN_COLS), _F32),
                        pltpu.VMEM((t, 2 * d), _F32),
                        pltpu.VMEM((CONV_CH // LANES, CONV_HALO + t, LANES), _F32),
                        pltpu.VMEM((N_POOL_GROUPS, POOL_HALO + t, LANES), _F32),
                        pltpu.VMEM((t, CONV_CH), _F32),
                        pltpu.VMEM((t, d), _BF16)],
        compiler_params=params,
        name="token_mix",
    )(x, x, mod, mod, w_in[0].astype(_BF16), w_dw_a[0], rows_a, w_pw_a[0].astype(_BF16),
      w_pool[0].astype(_BF16), rows_d, w_out[0].astype(_BF16))

    t = FFN_TILE
    tiles_per_seq, n_tiles, x_spec, mod_spec = tiling(t)
    out = pl.pallas_call(
        functools.partial(_ffn_kernel, n_tiles=n_tiles, tiles_per_seq=tiles_per_seq),
        out_shape=jax.ShapeDtypeStruct(x.shape, x.dtype),
        grid=(n_tiles + FFN_LAG,),
        in_specs=[x_spec(0), x_spec(FFN_LAG), mod_spec(0), mod_spec(FFN_LAG),
                  _resident((d, 2 * D_FF)), _resident((PACK_ROWS, 2 * D_FF)),
                  _resident((D_FF, d)), _resident((PACK_ROWS, d))],
        out_specs=x_spec(FFN_LAG),
        scratch_shapes=[pltpu.VMEM((2 * D_FF // LANES, FFN_HALO + t, LANES), _F32),
                        pltpu.VMEM((t, D_FF), _BF16)],
        compiler_params=params,
        name="channel_mix",
    )(x1, x1, mod, mod, w_up[0].astype(_BF16), rows_f, w_down[0].astype(_BF16), rows_d)
    return out
```
